```python
import jax, jax.numpy as jnp
from jax import lax
import numpy as np

D_MODEL = 1024
BATCH = 8
SEQ = 2048
DEPTH = 2
DEC_BATCH = 128
DEC_SEQ = 8
PAST_LEN = 16384
PAGE_SIZE = 128

N_MIXERS = 2
N_A_LAYERS = (DEPTH + 1) // 2
N_B_LAYERS = DEPTH // 2
CHUNK = 128
A_WIDTH = 2 * D_MODEL
A_GROUPS = 8
A_GROUP_DIM = A_WIDTH // A_GROUPS
B_WIDTH = D_MODEL
CONV_W = 3
N_GROUPS = 4
EXPERTS_PER_GROUP = 8
N_EXPERTS = N_GROUPS * EXPERTS_PER_GROUP
TOP_K = 2
D_EXPERT = D_MODEL // 2
N_MOD = 6
EPS = 1e-6

kernel_name = "hybrid_gmlp_shortconv_hmoe_step"


def rmsnorm(x, g):
    xf = x.astype(jnp.float32)
    y = xf * lax.rsqrt(jnp.mean(xf * xf, axis=-1, keepdims=True) + EPS)
    return (y * g.astype(jnp.float32)).astype(x.dtype)


def layernorm(x, g, b):
    xf = x.astype(jnp.float32)
    mu = jnp.mean(xf, axis=-1, keepdims=True)
    var = jnp.mean(jnp.square(xf - mu), axis=-1, keepdims=True)
    y = (xf - mu) * lax.rsqrt(var + EPS) * g.astype(jnp.float32) + b.astype(jnp.float32)
    return y.astype(x.dtype)


def chunk_gmlp(h, w_in, ln_g, ln_b, w_s, b_s, w_out):
    n, l, _ = h.shape
    z = jax.nn.gelu(h @ w_in)
    u, v = jnp.split(z, 2, axis=-1)
    v = layernorm(v, ln_g, ln_b)
    n_chunks = -(-l // CHUNK)
    pad = n_chunks * CHUNK - l
    vp = jnp.pad(v, ((0, 0), (0, pad), (0, 0))).reshape(n, n_chunks, CHUNK, A_GROUPS, A_GROUP_DIM)
    causal = jnp.tril(jnp.ones((CHUNK, CHUNK), dtype=bool))
    ws = jnp.where(causal, w_s, jnp.zeros_like(w_s)).astype(v.dtype)
    s = jnp.einsum('gts,ncsgd->nctgd', ws, vp) + b_s.T.astype(v.dtype)[None, None, :, :, None]
    s = s.reshape(n, n_chunks * CHUNK, A_WIDTH)[:, :l]
    return (u * s) @ w_out, v


def short_conv(h, prev, w_in, conv_w, w_out):
    l = h.shape[1]
    bg, cg, hx = jnp.split(h @ w_in, 3, axis=-1)
    z = cg * hx
    zc = jnp.concatenate([prev.astype(z.dtype), z], axis=1)
    conv = sum(conv_w[k] * zc[:, k:k + l] for k in range(CONV_W))
    return (bg * conv) @ w_out, zc[:, -(CONV_W - 1):]


def hier_moe(h, w_group, b_group, w_router, b_router, w_gate, w_up, w_down):
    n, l, d = h.shape
    t = h.reshape(n * l, d)
    tf = t.astype(jnp.float32)
    g_prob = jax.nn.softmax(tf @ w_group.astype(jnp.float32) + b_group.astype(jnp.float32), axis=-1)
    g_sel = jnp.argmax(g_prob, axis=-1)
    g_w = jnp.max(g_prob, axis=-1)
    e_logit = (tf @ w_router.astype(jnp.float32) + b_router.astype(jnp.float32)).reshape(-1, N_GROUPS, EXPERTS_PER_GROUP)
    e_logit = jnp.einsum('tge,tg->te', e_logit, jax.nn.one_hot(g_sel, N_GROUPS, dtype=jnp.float32))
    top_v, top_i = lax.top_k(e_logit, TOP_K)
    top_w = jax.nn.softmax(top_v, axis=-1) * g_w[:, None]
    e_idx = g_sel[:, None] * EXPERTS_PER_GROUP + top_i
    gates = jnp.sum(jax.nn.one_hot(e_idx, N_EXPERTS, dtype=jnp.float32) * top_w[..., None], axis=1).astype(t.dtype)

    def body(e, acc):
        hid = jax.nn.silu(t @ w_gate[e]) * (t @ w_up[e])
        return acc + gates[:, e][:, None] * (hid @ w_down[e])

    y = lax.fori_loop(0, N_EXPERTS, body, jnp.zeros_like(t))
    return y.reshape(n, l, d)


def forward(x, c, conv_prev, p):
    n = x.shape[0]
    cs = jax.nn.silu(c)
    conv_out, v_out = [], []
    for i in range(DEPTH):
        mod = (cs @ p['ada_w'][i] + p['ada_b'][i]).reshape(n, N_MOD, D_MODEL)[:, None]
        shift1, scale1, gate1, shift2, scale2, gate2 = [mod[:, :, k] for k in range(N_MOD)]
        h = rmsnorm(x, p['norm1_g'][i]) * (1 + scale1) + shift1
        j = i // N_MIXERS
        if i % N_MIXERS == 0:
            mix, v = chunk_gmlp(h, p['a_w_in'][j], p['a_ln_g'][j], p['a_ln_b'][j],
                                p['a_w_s'][j], p['a_b_s'][j], p['a_w_out'][j])
            v_out.append(v)
        else:
            mix, buf = short_conv(h, conv_prev[j], p['b_w_in'][j], p['b_conv_w'][j], p['b_w_out'][j])
            conv_out.append(buf)
        x = x + gate1 * mix
        h = rmsnorm(x, p['norm2_g'][i]) * (1 + scale2) + shift2
        x = x + gate2 * hier_moe(h, p['moe_w_group'][i], p['moe_b_group'][i], p['moe_w_router'][i],
                                 p['moe_b_router'][i], p['moe_w_gate'][i], p['moe_w_up'][i], p['moe_w_down'][i])
    return rmsnorm(x, p['final_g']), jnp.stack(conv_out), jnp.stack(v_out)


def setup_inputs(seed: int = 0) -> dict:
    key = jax.random.key(seed)
    ks = jax.random.split(key, 32)
    f32 = jnp.float32
    nrm = lambda k, shape, s: jax.random.normal(k, shape, f32) * s
    return {
        "x_prompt": nrm(ks[0], (BATCH, SEQ, D_MODEL), 1.0),
        "x_sample": nrm(ks[1], (DEC_BATCH, DEC_SEQ, D_MODEL), 1.0),
        "c_prompt": nrm(ks[2], (BATCH, D_MODEL), 1.0),
        "c_sample": nrm(ks[3], (DEC_BATCH, D_MODEL), 1.0),
        "state_conv": nrm(ks[4], (N_B_LAYERS, DEC_BATCH, CONV_W - 1, B_WIDTH), 1.0),
        "ada_w": nrm(ks[5], (DEPTH, D_MODEL, N_MOD * D_MODEL), 0.5 * D_MODEL ** -0.5),
        "ada_b": nrm(ks[6], (DEPTH, N_MOD * D_MODEL), 0.02),
        "norm1_g": 1.0 + nrm(ks[7], (DEPTH, D_MODEL), 0.02),
        "norm2_g": 1.0 + nrm(ks[8], (DEPTH, D_MODEL), 0.02),
        "final_g": 1.0 + nrm(ks[9], (D_MODEL,), 0.02),
        "a_w_in": nrm(ks[10], (N_A_LAYERS, D_MODEL, 2 * A_WIDTH), D_MODEL ** -0.5),
        "a_ln_g": 1.0 + nrm(ks[11], (N_A_LAYERS, A_WIDTH), 0.02),
        "a_ln_b": nrm(ks[12], (N_A_LAYERS, A_WIDTH), 0.02),
        "a_w_s": nrm(ks[13], (N_A_LAYERS, A_GROUPS, CHUNK, CHUNK), CHUNK ** -0.5),
        "a_b_s": 1.0 + nrm(ks[14], (N_A_LAYERS, A_GROUPS, CHUNK), 0.1),
        "a_w_out": nrm(ks[15], (N_A_LAYERS, A_WIDTH, D_MODEL), A_WIDTH ** -0.5),
        "b_w_in": nrm(ks[16], (N_B_LAYERS, D_MODEL, 3 * B_WIDTH), D_MODEL ** -0.5),
        "b_conv_w": nrm(ks[17], (N_B_LAYERS, CONV_W, B_WIDTH), CONV_W ** -0.5),
        "b_w_out": nrm(ks[18], (N_B_LAYERS, B_WIDTH, D_MODEL), B_WIDTH ** -0.5),
        "moe_w_group": nrm(ks[19], (DEPTH, D_MODEL, N_GROUPS), D_MODEL ** -0.5),
        "moe_b_group": nrm(ks[20], (DEPTH, N_GROUPS), 0.01),
        "moe_w_router": nrm(ks[21], (DEPTH, D_MODEL, N_EXPERTS), D_MODEL ** -0.5),
        "moe_b_router": nrm(ks[22], (DEPTH, N_EXPERTS), 0.01),
        "moe_w_gate": nrm(ks[23], (DEPTH, N_EXPERTS, D_MODEL, D_EXPERT), D_MODEL ** -0.5),
        "moe_w_up": nrm(ks[24], (DEPTH, N_EXPERTS, D_MODEL, D_EXPERT), D_MODEL ** -0.5),
        "moe_w_down": nrm(ks[25], (DEPTH, N_EXPERTS, D_EXPERT, D_MODEL), D_EXPERT ** -0.5),
    }


def reference(x_prompt, x_sample, c_prompt, c_sample, state_conv, ada_w, ada_b, norm1_g, norm2_g, final_g,
              a_w_in, a_ln_g, a_ln_b, a_w_s, a_b_s, a_w_out, b_w_in, b_conv_w, b_w_out,
              moe_w_group, moe_b_group, moe_w_router, moe_b_router, moe_w_gate, moe_w_up, moe_w_down):
    p = dict(ada_w=ada_w, ada_b=ada_b, norm1_g=norm1_g, norm2_g=norm2_g, final_g=final_g,
             a_w_in=a_w_in, a_ln_g=a_ln_g, a_ln_b=a_ln_b, a_w_s=a_w_s, a_b_s=a_b_s, a_w_out=a_w_out,
             b_w_in=b_w_in, b_conv_w=b_conv_w, b_w_out=b_w_out,
             moe_w_group=moe_w_group, moe_b_group=moe_b_group, moe_w_router=moe_w_router,
             moe_b_router=moe_b_router, moe_w_gate=moe_w_gate, moe_w_up=moe_w_up, moe_w_down=moe_w_down)
    conv_zero = jnp.zeros((N_B_LAYERS, x_prompt.shape[0], CONV_W - 1, B_WIDTH), x_prompt.dtype)
    y_prompt, conv_state_prompt, _ = forward(x_prompt, c_prompt, conv_zero, p)
    y_sample, conv_state_sample, chunk_v_sample = forward(x_sample, c_sample, state_conv, p)
    return (y_prompt, y_sample, conv_state_prompt, conv_state_sample, chunk_v_sample)
```

```python
import functools

import jax
import jax.numpy as jnp
from jax import lax
from jax.experimental import pallas as pl
from jax.experimental.pallas import tpu as pltpu

D_MODEL = 1024
CHUNK = 128
A_WIDTH = 2 * D_MODEL
A_GROUPS = 8
A_GROUP_DIM = A_WIDTH // A_GROUPS
B_WIDTH = D_MODEL
CONV_W = 3
N_GROUPS = 4
EXPERTS_PER_GROUP = 8
N_EXPERTS = N_GROUPS * EXPERTS_PER_GROUP
D_EXPERT = D_MODEL // 2
N_MOD = 6
EPS = 1e-6

LANES = 128
SUBLANES = 8
ROW_SLABS = D_MODEL // LANES
ROUTER_ROWS = SUBLANES + N_EXPERTS

TL = 256
TM = 256
TB = 512
VMEM_LIMIT = 56 * 1024 * 1024

F32 = jnp.float32
BF16 = jnp.bfloat16


def _dot(a, b):
    return jnp.dot(a, b, preferred_element_type=F32)


def _dot_nt(a, b):
    return lax.dot_general(a, b, (((1,), (1,)), ((), ())), preferred_element_type=F32)


def _sigmoid(x):
    return 1.0 / (1.0 + jnp.exp(-x))


def _gelu_tanh(x):
    c = 0.7978845608028654
    return 0.5 * x * (1.0 + jnp.tanh(c * (x + 0.044715 * (x * x * x))))


def _rms(x, g):
    return x * lax.rsqrt(jnp.mean(x * x, axis=-1, keepdims=True) + EPS) * g


def _store_row_chunks(ref, val, rows):
    for j in range(ROW_SLABS):
        ref[pl.ds(j, rows, stride=ROW_SLABS), :] = val[:, j * LANES:(j + 1) * LANES]


def _load_row_chunks(ref, rows):
    return jnp.concatenate(
        [ref[pl.ds(j, rows, stride=ROW_SLABS), :] for j in range(ROW_SLABS)], axis=1)


def _split_bf16(x):
    hi = x.astype(BF16)
    lo = (x - hi.astype(F32)).astype(BF16)
    return hi, lo


def _ada_kernel(c_ref, w_ref, b_ref, o_ref):
    c = c_ref[...]
    cs = (c * _sigmoid(c)).astype(BF16)
    o_ref[...] = _dot(cs, w_ref[...].astype(BF16)) + b_ref[...]


def _ada(c_all, ada_w, ada_b):
    n = c_all.shape[0]
    depth = ada_w.shape[0]
    bn = 1536
    nb = (N_MOD * D_MODEL) // bn
    return pl.pallas_call(
        _ada_kernel,
        out_shape=jax.ShapeDtypeStruct((depth, n, N_MOD * D_MODEL), F32),
        grid=(depth, nb),
        in_specs=[
            pl.BlockSpec((n, D_MODEL), lambda i, j: (0, 0)),
            pl.BlockSpec((None, D_MODEL, bn), lambda i, j: (i, 0, j)),
            pl.BlockSpec((None, 1, bn), lambda i, j: (i, 0, j)),
        ],
        out_specs=pl.BlockSpec((None, n, bn), lambda i, j: (i, 0, j)),
        compiler_params=pltpu.CompilerParams(
            dimension_semantics=("arbitrary", "arbitrary"), vmem_limit_bytes=VMEM_LIMIT),
        name="ada_mod",
    )(c_all, ada_w, ada_b.reshape(depth, 1, N_MOD * D_MODEL))


def _route(h2, rw_ref, rb_ref, rinfo_ref):
    tl = h2.shape[0]
    h_hi, h_lo = _split_bf16(h2)
    w_hi, w_lo = _split_bf16(rw_ref[...])
    logit = _dot_nt(w_hi, h_hi) + _dot_nt(w_hi, h_lo) + _dot_nt(w_lo, h_hi)
    bias = jnp.concatenate([rb_ref[...]] * (tl // LANES), axis=1)
    logit = logit + bias
    row = lax.broadcasted_iota(jnp.int32, (SUBLANES, tl), 0)
    neg = jnp.float32(-jnp.inf)

    gl = jnp.where(row < N_GROUPS, logit[0:SUBLANES], neg)
    gmax = jnp.max(gl, axis=0, keepdims=True)
    g_sel = jnp.min(jnp.where(gl == gmax, row, SUBLANES), axis=0, keepdims=True)
    g_w = 1.0 / jnp.sum(jnp.where(row < N_GROUPS, jnp.exp(gl - gmax), 0.0), axis=0, keepdims=True)

    el = jnp.zeros((EXPERTS_PER_GROUP, tl), F32)
    for g in range(N_GROUPS):
        lo = SUBLANES + g * EXPERTS_PER_GROUP
        el = jnp.where(g_sel == g, logit[lo:lo + EXPERTS_PER_GROUP], el)
    v1 = jnp.max(el, axis=0, keepdims=True)
    i1 = jnp.min(jnp.where(el == v1, row, SUBLANES), axis=0, keepdims=True)
    el2 = jnp.where(row == i1, neg, el)
    v2 = jnp.max(el2, axis=0, keepdims=True)
    i2 = jnp.min(jnp.where(el2 == v2, row, SUBLANES), axis=0, keepdims=True)
    d = jnp.exp(v2 - v1)
    w1 = (1.0 / (1.0 + d)) * g_w
    w2 = (d / (1.0 + d)) * g_w
    e1 = (g_sel * EXPERTS_PER_GROUP + i1).astype(F32)
    e2 = (g_sel * EXPERTS_PER_GROUP + i2).astype(F32)
    rinfo_ref[...] = jnp.where(row == 0, e1, jnp.where(row == 1, e2, jnp.where(
        row == 2, w1, jnp.where(row == 3, w2, 0.0))))


def _mixer_tail(x, mix, mod_ref, n2g_ref, rw_ref, rb_ref, xmid_ref, h2_ref, rinfo_ref):
    gate1 = mod_ref[:, 2 * D_MODEL:3 * D_MODEL]
    shift2 = mod_ref[:, 3 * D_MODEL:4 * D_MODEL]
    scale2 = mod_ref[:, 4 * D_MODEL:5 * D_MODEL]
    xmid = x + gate1 * mix
    xmid_ref[...] = xmid
    h2 = _rms(xmid, n2g_ref[...]) * (1.0 + scale2) + shift2
    _store_row_chunks(h2_ref, h2, x.shape[0])
    _route(h2, rw_ref, rb_ref, rinfo_ref)


def _premix(x, mod_ref, n1g_ref):
    shift1 = mod_ref[:, 0:D_MODEL]
    scale1 = mod_ref[:, D_MODEL:2 * D_MODEL]
    return (_rms(x, n1g_ref[...]) * (1.0 + scale1) + shift1).astype(BF16)


def _gmlp_kernel(x_ref, mod_ref, n1g_ref, n2g_ref, win_ref, lng_ref, lnb_ref, ws_ref, bs_ref,
                 wout_ref, rw_ref, rb_ref, *rest, seg, emit_v):
    if emit_v:
        xmid_ref, h2_ref, rinfo_ref, vout_ref, v_scr, p_scr = rest
    else:
        xmid_ref, h2_ref, rinfo_ref, v_scr, p_scr = rest
    tl = x_ref.shape[0]
    x = x_ref[...]
    hb = _premix(x, mod_ref, n1g_ref)

    cw = 512
    for c in range(A_WIDTH // cw):
        lo = A_WIDTH + c * cw
        v_scr[:, c * cw:(c + 1) * cw] = _gelu_tanh(_dot(hb, win_ref[:, lo:lo + cw]))
    v = v_scr[...]
    mu = jnp.mean(v, axis=-1, keepdims=True)
    vc = v - mu
    var = jnp.mean(vc * vc, axis=-1, keepdims=True)
    vn = vc * lax.rsqrt(var + EPS) * lng_ref[...] + lnb_ref[...]
    if emit_v:
        vout_ref[...] = vn
    vb = vn.astype(BF16)

    r = lax.broadcasted_iota(jnp.int32, (tl, tl), 0)
    c_ = lax.broadcasted_iota(jnp.int32, (tl, tl), 1)
    shift = seg.bit_length() - 1
    keep = ((r >> shift) == (c_ >> shift)) & (c_ <= r)
    for g in range(A_GROUPS):
        lo = g * A_GROUP_DIM
        m = jnp.where(keep, ws_ref[g], 0.0).astype(BF16)
        bias = jnp.concatenate([bs_ref[g]] * (A_GROUP_DIM // LANES), axis=1)
        s = _dot(m, vb[:, lo:lo + A_GROUP_DIM]) + bias
        u = _gelu_tanh(_dot(hb, win_ref[:, lo:lo + A_GROUP_DIM]))
        p_scr[:, lo:lo + A_GROUP_DIM] = (u * s).astype(BF16)
    mix = _dot(p_scr[...], wout_ref[...])
    _mixer_tail(x, mix, mod_ref, n2g_ref, rw_ref, rb_ref, xmid_ref, h2_ref, rinfo_ref)


def _mod_spec(per_row, tiles_per_seq):
    if per_row:
        return pl.BlockSpec((TL, N_MOD * D_MODEL), lambda i: (i, 0))
    return pl.BlockSpec((None, 1, N_MOD * D_MODEL), lambda i: (i // tiles_per_seq, 0, 0))


def _const_spec(shape):
    nd = len(shape)
    return pl.BlockSpec(shape, lambda i: (0,) * nd)


def _mixer_out(t_rows, extra):
    shapes = [jax.ShapeDtypeStruct((t_rows, D_MODEL), F32),
              jax.ShapeDtypeStruct((t_rows * ROW_SLABS, LANES), F32),
              jax.ShapeDtypeStruct((SUBLANES, t_rows), F32)]
    specs = [pl.BlockSpec((TL, D_MODEL), lambda i: (i, 0)),
             pl.BlockSpec((TL * ROW_SLABS, LANES), lambda i: (i, 0)),
             pl.BlockSpec((SUBLANES, TL), lambda i: (0, i))]
    return shapes + [s for s, _ in extra], specs + [b for _, b in extra]


def _gmlp(x, mod, per_row, seq_len, n1g, n2g, win, lng, lnb, ws_t, bs_t, wout, rw, rb, seg, emit_v):
    t_rows = x.shape[0]
    extra = []
    if emit_v:
        extra.append((jax.ShapeDtypeStruct((t_rows, A_WIDTH), F32),
                      pl.BlockSpec((TL, A_WIDTH), lambda i: (i, 0))))
    out_shape, out_specs = _mixer_out(t_rows, extra)
    return pl.pallas_call(
        functools.partial(_gmlp_kernel, seg=seg, emit_v=emit_v),
        out_shape=out_shape,
        grid=(t_rows // TL,),
        in_specs=[
            pl.BlockSpec((TL, D_MODEL), lambda i: (i, 0)),
            _mod_spec(per_row, seq_len // TL),
            _const_spec((1, D_MODEL)), _const_spec((1, D_MODEL)),
            _const_spec((D_MODEL, 2 * A_WIDTH)),
            _const_spec((1, A_WIDTH)), _const_spec((1, A_WIDTH)),
            _const_spec((A_GROUPS, TL, TL)), _const_spec((A_GROUPS, TL, LANES)),
            _const_spec((A_WIDTH, D_MODEL)),
            _const_spec((ROUTER_ROWS, D_MODEL)), _const_spec((ROUTER_ROWS, LANES)),
        ],
        out_specs=out_specs,
        scratch_shapes=[pltpu.VMEM((TL, A_WIDTH), F32), pltpu.VMEM((TL, A_WIDTH), BF16)],
        compiler_params=pltpu.CompilerParams(
            dimension_semantics=("arbitrary",), vmem_limit_bytes=VMEM_LIMIT),
        name="gmlp_mixer",
    )(x, mod, n1g, n2g, win, lng, lnb, ws_t, bs_t, wout, rw, rb)


def _conv_kernel(x_ref, mod_ref, n1g_ref, n2g_ref, win_ref, cw_ref, wout_ref, rw_ref, rb_ref,
                 *rest, tiles_per_seq, per_row):
    if per_row:
        p1_ref, p2_ref, xmid_ref, h2_ref, rinfo_ref, z_ref, zbuf = rest
    else:
        xmid_ref, h2_ref, rinfo_ref, z_ref, zbuf = rest
    tl = x_ref.shape[0]
    x = x_ref[...]
    hb = _premix(x, mod_ref, n1g_ref)
    bg = _dot(hb, win_ref[:, 0:B_WIDTH])
    cg = _dot(hb, win_ref[:, B_WIDTH:2 * B_WIDTH])
    hx = _dot(hb, win_ref[:, 2 * B_WIDTH:3 * B_WIDTH])
    z = cg * hx

    if per_row:
        zbuf[0:SUBLANES, :] = jnp.zeros((SUBLANES, B_WIDTH), F32)
    else:
        @pl.when(pl.program_id(0) % tiles_per_seq == 0)
        def _():
            zbuf[0:SUBLANES, :] = jnp.zeros((SUBLANES, B_WIDTH), F32)
    zbuf[SUBLANES:SUBLANES + tl, :] = z
    sh1 = zbuf[SUBLANES - 1:SUBLANES - 1 + tl, :]
    sh2 = zbuf[SUBLANES - 2:SUBLANES - 2 + tl, :]
    if per_row:
        pos = lax.broadcasted_iota(jnp.int32, (tl, B_WIDTH), 0) & (SUBLANES - 1)
        sh1 = jnp.where(pos == 0, p1_ref[...], sh1)
        sh2 = jnp.where(pos < 2, p2_ref[...], sh2)
        z_ref[...] = z
    else:
        zbuf[0:SUBLANES, :] = z[tl - SUBLANES:tl, :]
        z_ref[...] = z[tl - SUBLANES:tl, :]
    conv = cw_ref[0:1, :] * sh2 + cw_ref[1:2, :] * sh1 + cw_ref[2:3, :] * z
    mix = _dot((bg * conv).astype(BF16), wout_ref[...])
    _mixer_tail(x, mix, mod_ref, n2g_ref, rw_ref, rb_ref, xmid_ref, h2_ref, rinfo_ref)


def _conv(x, mod, per_row, seq_len, n1g, n2g, win, cw, wout, rw, rb, prev1=None, prev2=None):
    t_rows = x.shape[0]
    nt = t_rows // TL
    if per_row:
        extra = [(jax.ShapeDtypeStruct((t_rows, B_WIDTH), F32),
                  pl.BlockSpec((TL, B_WIDTH), lambda i: (i, 0)))]
    else:
        extra = [(jax.ShapeDtypeStruct((nt * SUBLANES, B_WIDTH), F32),
                  pl.BlockSpec((SUBLANES, B_WIDTH), lambda i: (i, 0)))]
    out_shape, out_specs = _mixer_out(t_rows, extra)
    in_specs = [
        pl.BlockSpec((TL, D_MODEL), lambda i: (i, 0)),
        _mod_spec(per_row, seq_len // TL),
        _const_spec((1, D_MODEL)), _const_spec((1, D_MODEL)),
        _const_spec((D_MODEL, 3 * B_WIDTH)),
        _const_spec((CONV_W, B_WIDTH)),
        _const_spec((B_WIDTH, D_MODEL)),
        _const_spec((ROUTER_ROWS, D_MODEL)), _const_spec((ROUTER_ROWS, LANES)),
    ]
    args = [x, mod, n1g, n2g, win, cw, wout, rw, rb]
    if per_row:
        in_specs += [pl.BlockSpec((TL, B_WIDTH), lambda i: (i, 0))] * 2
        args += [prev1, prev2]
    return pl.pallas_call(
        functools.partial(_conv_kernel, tiles_per_seq=max(seq_len // TL, 1), per_row=per_row),
        out_shape=out_shape,
        grid=(nt,),
        in_specs=in_specs,
        out_specs=out_specs,
        scratch_shapes=[pltpu.VMEM((TL + SUBLANES, B_WIDTH), F32)],
        compiler_params=pltpu.CompilerParams(
            dimension_semantics=("arbitrary",), vmem_limit_bytes=VMEM_LIMIT),
        name="conv_mixer",
    )(*args)


def _plan_kernel(rinfo_ref, pos_ref, cnt_ref, total, run, offs):
    phase = pl.program_id(0)
    b = pl.program_id(1)
    nb = pl.num_programs(1)
    tb = rinfo_ref.shape[1]
    e1 = rinfo_ref[0:1, :]
    e2 = rinfo_ref[1:2, :]
    erow = lax.broadcasted_iota(jnp.int32, (N_EXPERTS, tb), 0).astype(F32)
    hit1 = e1 == erow
    hit2 = e2 == erow
    member = jnp.where(hit1 | hit2, 1.0, 0.0)
    blk_cnt = jnp.broadcast_to(jnp.sum(member, axis=1, keepdims=True), (N_EXPERTS, LANES))

    @pl.when(phase == 0)
    def _():
        @pl.when(b == 0)
        def _():
            total[...] = jnp.zeros_like(total)
        total[...] += blk_cnt

        @pl.when(b == nb - 1)
        def _():
            cnt_ref[...] = total[...]

    @pl.when(phase == 1)
    def _():
        @pl.when(b == 0)
        def _():
            run[...] = jnp.zeros_like(run)
            padded = jnp.floor((total[...] + (TM - 1)) * (1.0 / TM)) * TM
            sub = lax.broadcasted_iota(jnp.int32, (N_EXPERTS, LANES), 0)
            lane = lax.broadcasted_iota(jnp.int32, (N_EXPERTS, LANES), 1)
            start_by_lane = jnp.sum(jnp.where(sub < lane, padded, 0.0), axis=0, keepdims=True)
            start = jnp.sum(jnp.where(lane == sub, start_by_lane, 0.0), axis=1, keepdims=True)
            offs[...] = jnp.broadcast_to(start, (N_EXPERTS, LANES))

        s_ = lax.broadcasted_iota(jnp.int32, (tb, tb), 0)
        t_ = lax.broadcasted_iota(jnp.int32, (tb, tb), 1)
        upper = jnp.where(s_ < t_, 1.0, 0.0).astype(BF16)
        before = _dot(member.astype(BF16), upper)
        dest = before + jnp.concatenate([run[...] + offs[...]] * (tb // LANES), axis=1)
        pos1 = jnp.sum(jnp.where(hit1, dest, 0.0), axis=0, keepdims=True)
        pos2 = jnp.sum(jnp.where(hit2, dest, 0.0), axis=0, keepdims=True)
        row = lax.broadcasted_iota(jnp.int32, (SUBLANES, tb), 0)
        pos_ref[...] = jnp.where(row == 0, pos1, jnp.where(row == 1, pos2, 0.0)).astype(jnp.int32)
        run[...] += blk_cnt


def _plan(rinfo):
    t = rinfo.shape[1]
    nb = t // TB
    return pl.pallas_call(
        _plan_kernel,
        out_shape=[jax.ShapeDtypeStruct((SUBLANES, t), jnp.int32),
                   jax.ShapeDtypeStruct((N_EXPERTS, LANES), F32)],
        grid=(2, nb),
        in_specs=[pl.BlockSpec((SUBLANES, TB), lambda p, b: (0, b))],
        out_specs=[pl.BlockSpec((SUBLANES, TB), lambda p, b: (0, b * p)),
                   pl.BlockSpec((N_EXPERTS, LANES), lambda p, b: (0, 0))],
        scratch_shapes=[pltpu.VMEM((N_EXPERTS, LANES), F32)] * 3,
        compiler_params=pltpu.CompilerParams(
            dimension_semantics=("arbitrary", "arbitrary"), vmem_limit_bytes=VMEM_LIMIT),
        name="moe_plan",
    )(rinfo)


def _row_copy(src_ref, src_row, dst_ref, dst_row, sem):
    return pltpu.make_async_copy(
        src_ref.at[pl.ds(pl.multiple_of(src_row * ROW_SLABS, ROW_SLABS), ROW_SLABS)],
        dst_ref.at[pl.ds(pl.multiple_of(dst_row * ROW_SLABS, ROW_SLABS), ROW_SLABS)],
        sem)


def _dispatch_kernel(pos_ref, h_ref, *rest):
    xs_ref, sem = rest[-2:]
    tl = h_ref.shape[0] // ROW_SLABS

    def issue(i, carry):
        for k in range(2):
            _row_copy(h_ref, i, xs_ref, pos_ref[0, k * tl + i], sem).start()
        return carry

    def drain(i, carry):
        for k in range(2):
            _row_copy(h_ref, i, xs_ref, pos_ref[0, k * tl + i], sem).wait()
        return carry

    lax.fori_loop(0, tl, issue, 0)
    lax.fori_loop(0, tl, drain, 0)


def _dispatch(pos_tiles, h2, xs_prev, sorted_rows):
    nt = h2.shape[0] // (TL * ROW_SLABS)
    in_specs = [
        pl.BlockSpec((None, 1, 2 * TL), lambda i: (i, 0, 0), memory_space=pltpu.SMEM),
        pl.BlockSpec((TL * ROW_SLABS, LANES), lambda i: (i, 0)),
    ]
    args = [pos_tiles, h2]
    aliases = {}
    if xs_prev is not None:
        in_specs.append(pl.BlockSpec(memory_space=pl.ANY))
        args.append(xs_prev)
        aliases = {2: 0}
    return pl.pallas_call(
        _dispatch_kernel,
        out_shape=jax.ShapeDtypeStruct((sorted_rows * ROW_SLABS, LANES), F32),
        grid=(nt,),
        in_specs=in_specs,
        out_specs=pl.BlockSpec(memory_space=pl.ANY),
        scratch_shapes=[pltpu.SemaphoreType.DMA(())],
        input_output_aliases=aliases,
        compiler_params=pltpu.CompilerParams(
            dimension_semantics=("arbitrary",), vmem_limit_bytes=VMEM_LIMIT,
            has_side_effects=True),
        name="moe_dispatch",
    )(*args)


def _expert_kernel(te_ref, nv_ref, blk_ref, xs_ref, wg_ref, wu_ref, wd_ref, ys_ref):
    j = pl.program_id(0)
    nvalid = nv_ref[j]

    @pl.when(nvalid > 0)
    def _():
        x = _load_row_chunks(xs_ref, TM)
        row = lax.broadcasted_iota(jnp.int32, (TM, D_MODEL), 0)
        xb = jnp.where(row < nvalid, x, 0.0).astype(BF16)
        hg = _dot(xb, wg_ref[...].astype(BF16))
        hu = _dot(xb, wu_ref[...].astype(BF16))
        hid = (hg * _sigmoid(hg) * hu).astype(BF16)
        y = _dot(hid, wd_ref[...].astype(BF16))
        _store_row_chunks(ys_ref, y, TM)

    @pl.when(nvalid == 0)
    def _():
        ys_ref[...] = jnp.zeros(ys_ref.shape, F32)


def _experts(layer, tile_expert, tile_valid, tile_block, xs, w_gate, w_up, w_down):
    ntm = tile_expert.shape[0]
    rows_spec = pl.BlockSpec((TM * ROW_SLABS, LANES), lambda j, te, nv, blk: (blk[j], 0))
    out_spec = pl.BlockSpec((TM * ROW_SLABS, LANES), lambda j, te, nv, blk: (j, 0))
    return pl.pallas_call(
        _expert_kernel,
        out_shape=jax.ShapeDtypeStruct(xs.shape, F32),
        grid_spec=pltpu.PrefetchScalarGridSpec(
            num_scalar_prefetch=3,
            grid=(ntm,),
            in_specs=[
                rows_spec,
                pl.BlockSpec((None, None, D_MODEL, D_EXPERT), lambda j, te, nv, blk: (layer, te[j], 0, 0)),
                pl.BlockSpec((None, None, D_MODEL, D_EXPERT), lambda j, te, nv, blk: (layer, te[j], 0, 0)),
                pl.BlockSpec((None, None, D_EXPERT, D_MODEL), lambda j, te, nv, blk: (layer, te[j], 0, 0)),
            ],
            out_specs=out_spec,
        ),
        compiler_params=pltpu.CompilerParams(
            dimension_semantics=("arbitrary",), vmem_limit_bytes=VMEM_LIMIT),
        name="moe_experts",
    )(tile_expert, tile_valid, tile_block, xs, w_gate, w_up, w_down)


def _combine_kernel(pos_ref, xmid_ref, gate_ref, rinfo_ref, fg_ref, ys_ref, out_ref, buf0, buf1, sem,
                    *, final):
    tl = xmid_ref.shape[0]
    bufs = (buf0, buf1)

    def issue(i, carry):
        for k in range(2):
            _row_copy(ys_ref, pos_ref[0, k * tl + i], bufs[k], i, sem).start()
        return carry

    def drain(i, carry):
        for k in range(2):
            _row_copy(ys_ref, pos_ref[0, k * tl + i], bufs[k], i, sem).wait()
        return carry

    lax.fori_loop(0, tl, issue, 0)
    lax.fori_loop(0, tl, drain, 0)

    moe = jnp.zeros((tl, D_MODEL), F32)
    for k in range(2):
        w_rows = jnp.transpose(jnp.broadcast_to(rinfo_ref[2 + k:3 + k, :], (LANES, tl)))
        w_full = jnp.concatenate([w_rows] * ROW_SLABS, axis=1)
        moe = moe + w_full * _load_row_chunks(bufs[k], tl)
    xnew = xmid_ref[...] + gate_ref[...] * moe
    if final:
        xnew = _rms(xnew, fg_ref[...])
    out_ref[...] = xnew


def _combine(pos_tiles, xmid, mod, per_row, seq_len, rinfo, final_g, ys, final):
    t_rows = xmid.shape[0]
    if per_row:
        gate_spec = pl.BlockSpec((TL, D_MODEL), lambda i: (i, N_MOD - 1))
    else:
        tiles_per_seq = seq_len // TL
        gate_spec = pl.BlockSpec((None, 1, D_MODEL), lambda i: (i // tiles_per_seq, 0, N_MOD - 1))
    return pl.pallas_call(
        functools.partial(_combine_kernel, final=final),
        out_shape=jax.ShapeDtypeStruct((t_rows, D_MODEL), F32),
        grid=(t_rows // TL,),
        in_specs=[
            pl.BlockSpec((None, 1, 2 * TL), lambda i: (i, 0, 0), memory_space=pltpu.SMEM),
            pl.BlockSpec((TL, D_MODEL), lambda i: (i, 0)),
            gate_spec,
            pl.BlockSpec((SUBLANES, TL), lambda i: (0, i)),
            _const_spec((1, D_MODEL)),
            pl.BlockSpec(memory_space=pl.ANY),
        ],
        out_specs=pl.BlockSpec((TL, D_MODEL), lambda i: (i, 0)),
        scratch_shapes=[pltpu.VMEM((TL * ROW_SLABS, LANES), F32),
                        pltpu.VMEM((TL * ROW_SLABS, LANES), F32),
                        pltpu.SemaphoreType.DMA(())],
        compiler_params=pltpu.CompilerParams(
            dimension_semantics=("arbitrary",), vmem_limit_bytes=VMEM_LIMIT),
        name="moe_combine",
    )(pos_tiles, xmid, mod, rinfo, final_g, ys)


def _pos_tiles(pos, lo, n):
    p = pos[0:2, lo:lo + n].reshape(2, n // TL, TL)
    return jnp.transpose(p, (1, 0, 2)).reshape(n // TL, 1, 2 * TL)


def _tile_tables(counts, ntm):
    cnt = counts[:, 0].astype(jnp.int32)
    tiles = (cnt + (TM - 1)) // TM
    ends = jnp.cumsum(tiles)
    starts = ends - tiles
    n_used = ends[-1]
    j = jnp.arange(ntm, dtype=jnp.int32)
    jc = jnp.minimum(j, jnp.maximum(n_used - 1, 0))
    te = jnp.minimum(jnp.sum((ends[None, :] <= jc[:, None]).astype(jnp.int32), axis=1), N_EXPERTS - 1)
    local = jc - starts[te]
    nvalid = jnp.where(j < n_used, jnp.clip(cnt[te] - local * TM, 0, TM), 0)
    return te.astype(jnp.int32), nvalid.astype(jnp.int32), jc.astype(jnp.int32)


def _router_params(w_group, b_group, w_router, b_router):
    rw = jnp.zeros((ROUTER_ROWS, D_MODEL), F32)
    rw = rw.at[0:N_GROUPS].set(w_group.T).at[SUBLANES:].set(w_router.T)
    rb = jnp.zeros((ROUTER_ROWS,), F32)
    rb = rb.at[0:N_GROUPS].set(b_group).at[SUBLANES:].set(b_router)
    return rw, jnp.broadcast_to(rb[:, None], (ROUTER_ROWS, LANES))


def kernel(x_prompt, x_sample, c_prompt, c_sample, state_conv, ada_w, ada_b, norm1_g, norm2_g, final_g,
           a_w_in, a_ln_g, a_ln_b, a_w_s, a_b_s, a_w_out, b_w_in, b_conv_w, b_w_out,
           moe_w_group, moe_b_group, moe_w_router, moe_b_router, moe_w_gate, moe_w_up, moe_w_down):
    nb_p, seq_p, _ = x_prompt.shape
    nb_s, seq_s, _ = x_sample.shape
    tp = nb_p * seq_p
    ts = nb_s * seq_s
    t_all = tp + ts
    assert seq_p % TL == 0 and TL % seq_s == 0 and ts % TL == 0 and t_all % TB == 0
    assert seq_p % CHUNK == 0 and TL % CHUNK == 0 and seq_s <= CHUNK and seq_s == SUBLANES
    ntm = (2 * t_all) // TM + N_EXPERTS
    sorted_rows = ntm * TM

    mod = _ada(jnp.concatenate([c_prompt, c_sample], axis=0), ada_w, ada_b)
    fg = final_g.reshape(1, D_MODEL)

    xp = x_prompt.reshape(tp, D_MODEL)
    xs_ = x_sample.reshape(ts, D_MODEL)
    conv_p = conv_s = v_s = None
    for layer in range(2):
        mod_p = mod[layer, :nb_p].reshape(nb_p, 1, N_MOD * D_MODEL)
        mod_s = jnp.repeat(mod[layer, nb_p:], seq_s, axis=0)
        n1g = norm1_g[layer].reshape(1, D_MODEL)
        n2g = norm2_g[layer].reshape(1, D_MODEL)
        rw, rb = _router_params(moe_w_group[layer], moe_b_group[layer], moe_w_router[layer], moe_b_router[layer])
        if layer == 0:
            win = a_w_in[0].astype(BF16)
            wout = a_w_out[0].astype(BF16)
            lng = a_ln_g[0].reshape(1, A_WIDTH)
            lnb = a_ln_b[0].reshape(1, A_WIDTH)
            ws_p = jnp.tile(a_w_s[0], (1, TL // CHUNK, TL // CHUNK))
            ws_s = jnp.tile(a_w_s[0][:, :seq_s, :seq_s], (1, TL // seq_s, TL // seq_s))
            bs_p = jnp.broadcast_to(jnp.tile(a_b_s[0], (1, TL // CHUNK))[:, :, None], (A_GROUPS, TL, LANES))
            bs_s = jnp.broadcast_to(jnp.tile(a_b_s[0][:, :seq_s], (1, TL // seq_s))[:, :, None],
                                    (A_GROUPS, TL, LANES))
            xmid_p, h2_p, ri_p = _gmlp(xp, mod_p, False, seq_p, n1g, n2g, win, lng, lnb, ws_p, bs_p, wout,
                                       rw, rb, CHUNK, False)
            xmid_s, h2_s, ri_s, v_s = _gmlp(xs_, mod_s, True, seq_s, n1g, n2g, win, lng, lnb, ws_s, bs_s, wout,
                                            rw, rb, seq_s, True)
        else:
            win = b_w_in[0].astype(BF16)
            wout = b_w_out[0].astype(BF16)
            prev = state_conv[0]
            prev1 = jnp.repeat(prev[:, 1], seq_s, axis=0)
            first = (jnp.arange(ts) % seq_s == 0)[:, None]
            prev2 = jnp.where(first, jnp.repeat(prev[:, 0], seq_s, axis=0), prev1)
            xmid_p, h2_p, ri_p, zt_p = _conv(xp, mod_p, False, seq_p, n1g, n2g, win, b_conv_w[0], wout, rw, rb)
            xmid_s, h2_s, ri_s, z_s = _conv(xs_, mod_s, True, seq_s, n1g, n2g, win, b_conv_w[0], wout, rw, rb,
                                            prev1, prev2)
            conv_p = zt_p.reshape(nb_p, seq_p // TL, SUBLANES, B_WIDTH)[:, -1, SUBLANES - (CONV_W - 1):]
            conv_s = z_s.reshape(nb_s, seq_s, B_WIDTH)[:, seq_s - (CONV_W - 1):]

        rinfo = jnp.concatenate([ri_p, ri_s], axis=1)
        pos, counts = _plan(rinfo)
        te, nvalid, blk = _tile_tables(counts, ntm)
        pos_p = _pos_tiles(pos, 0, tp)
        pos_s = _pos_tiles(pos, tp, ts)
        rows = jnp.zeros((sorted_rows * ROW_SLABS, LANES), F32)
        rows = _dispatch(pos_p, h2_p, rows, sorted_rows)
        rows = _dispatch(pos_s, h2_s, rows, sorted_rows)
        ys = _experts(layer, te, nvalid, blk, rows, moe_w_gate, moe_w_up, moe_w_down)
        final = layer == 1
        xp = _combine(pos_p, xmid_p, mod_p, False, seq_p, ri_p, fg, ys, final)
        xs_ = _combine(pos_s, xmid_s, mod_s, True, seq_s, ri_s, fg, ys, final)

    return (xp.reshape(nb_p, seq_p, D_MODEL), xs_.reshape(nb_s, seq_s, D_MODEL),
            conv_p[None], conv_s[None], v_s.reshape(nb_s, seq_s, A_WIDTH)[None])
```

```python
import functools

import jax
import jax.numpy as jnp
from jax import lax
from jax.experimental import pallas as pl
from jax.experimental.pallas import tpu as pltpu

D_MODEL = 1024
CHUNK = 128
A_WIDTH = 2 * D_MODEL
A_GROUPS = 8
A_GROUP_DIM = A_WIDTH // A_GROUPS
B_WIDTH = D_MODEL
CONV_W = 3
N_GROUPS = 4
EXPERTS_PER_GROUP = 8
N_EXPERTS = N_GROUPS * EXPERTS_PER_GROUP
D_EXPERT = D_MODEL // 2
N_MOD = 6
EPS = 1e-6

LANES = 128
SUBLANES = 8
ROW_SLABS = D_MODEL // LANES
PACK_SLABS = ROW_SLABS // 2
ROUTER_ROWS = SUBLANES + N_EXPERTS

TL = 256
TM = 256
TB = 512
CODE_SHIFT = 16
VMEM_LIMIT = 56 * 1024 * 1024
EXPERT_VMEM_LIMIT = 62 * 1024 * 1024

F32 = jnp.float32
BF16 = jnp.bfloat16
I32 = jnp.int32
U32 = jnp.uint32


def _dot(a, b):
    return jnp.dot(a, b, preferred_element_type=F32)


def _dot_nt(a, b):
    return lax.dot_general(a, b, (((1,), (1,)), ((), ())), preferred_element_type=F32)


def _sigmoid(x):
    return 1.0 / (1.0 + jnp.exp(-x))


def _gelu_tanh(x):
    c = 0.7978845608028654
    return 0.5 * x * (1.0 + jnp.tanh(c * (x + 0.044715 * (x * x * x))))


def _rms(x, g):
    return x * lax.rsqrt(jnp.mean(x * x, axis=-1, keepdims=True) + EPS) * g


def _store_row_chunks(ref, val, rows, base=0):
    for j in range(ROW_SLABS):
        ref[pl.ds(base + j, rows, stride=ROW_SLABS), :] = val[:, j * LANES:(j + 1) * LANES]


def _load_row_chunks(ref, rows):
    return jnp.concatenate(
        [ref[pl.ds(j, rows, stride=ROW_SLABS), :] for j in range(ROW_SLABS)], axis=1)


def _store_packed_rows(ref, val, rows):
    for i in range(PACK_SLABS):
        lo = val[:, (2 * i) * LANES:(2 * i + 1) * LANES]
        hi = val[:, (2 * i + 1) * LANES:(2 * i + 2) * LANES]
        ref[pl.ds(i, rows, stride=PACK_SLABS), :] = pltpu.pack_elementwise([lo, hi], packed_dtype=BF16)


def _load_packed_rows(ref, rows):
    parts = []
    for i in range(PACK_SLABS):
        words = ref[pl.ds(i, rows, stride=PACK_SLABS), :]
        for half in range(2):
            parts.append(pltpu.unpack_elementwise(words, index=half, packed_dtype=BF16, unpacked_dtype=F32))
    return jnp.concatenate(parts, axis=1)


def _split_bf16(x):
    hi = x.astype(BF16)
    lo = (x - hi.astype(F32)).astype(BF16)
    return hi, lo


def _seq_rows(ref, lo, rows_per_seq):
    m = ref[:, lo:lo + D_MODEL]
    if m.shape[0] == 1:
        return m
    return jnp.concatenate(
        [jnp.broadcast_to(m[s:s + 1, :], (rows_per_seq, D_MODEL)) for s in range(m.shape[0])], axis=0)


def _ada_kernel(c_ref, w_ref, b_ref, o_ref):
    c = c_ref[...]
    cs = (c * _sigmoid(c)).astype(BF16)
    o_ref[...] = _dot(cs, w_ref[...].astype(BF16)) + b_ref[...]


def _ada(c_all, ada_w, ada_b):
    n = c_all.shape[0]
    depth = ada_w.shape[0]
    bn = 1536
    nb = (N_MOD * D_MODEL) // bn
    return pl.pallas_call(
        _ada_kernel,
        out_shape=jax.ShapeDtypeStruct((depth, n, N_MOD * D_MODEL), F32),
        grid=(depth, nb),
        in_specs=[
            pl.BlockSpec((n, D_MODEL), lambda i, j: (0, 0)),
            pl.BlockSpec((None, D_MODEL, bn), lambda i, j: (i, 0, j)),
            pl.BlockSpec((None, 1, bn), lambda i, j: (i, 0, j)),
        ],
        out_specs=pl.BlockSpec((None, n, bn), lambda i, j: (i, 0, j)),
        compiler_params=pltpu.CompilerParams(
            dimension_semantics=("arbitrary", "arbitrary"), vmem_limit_bytes=VMEM_LIMIT),
        name="ada_mod",
    )(c_all, ada_w, ada_b.reshape(depth, 1, N_MOD * D_MODEL))


def _moe_residual(xmid_ref, gate, rinfo_ref, o0_ref, o1_ref):
    tl = xmid_ref.shape[0]
    moe = jnp.zeros((tl, D_MODEL), F32)
    for k, o_ref in enumerate((o0_ref, o1_ref)):
        w_rows = jnp.transpose(jnp.broadcast_to(rinfo_ref[2 + k:3 + k, :], (LANES, tl)))
        w_full = jnp.concatenate([w_rows] * ROW_SLABS, axis=1)
        moe = moe + w_full * _load_row_chunks(o_ref, tl)
    return xmid_ref[...] + gate * moe


def _route(h2, rw_ref, rb_ref, rinfo_ref):
    tl = h2.shape[0]
    h_hi, h_lo = _split_bf16(h2)
    w_hi, w_lo = _split_bf16(rw_ref[...])
    logit = _dot_nt(w_hi, h_hi) + _dot_nt(w_hi, h_lo) + _dot_nt(w_lo, h_hi)
    bias = jnp.concatenate([rb_ref[...]] * (tl // LANES), axis=1)
    logit = logit + bias
    row = lax.broadcasted_iota(I32, (SUBLANES, tl), 0)
    neg = jnp.float32(-jnp.inf)

    gl = jnp.where(row < N_GROUPS, logit[0:SUBLANES], neg)
    gmax = jnp.max(gl, axis=0, keepdims=True)
    g_sel = jnp.min(jnp.where(gl == gmax, row, SUBLANES), axis=0, keepdims=True)
    g_w = 1.0 / jnp.sum(jnp.where(row < N_GROUPS, jnp.exp(gl - gmax), 0.0), axis=0, keepdims=True)

    el = jnp.zeros((EXPERTS_PER_GROUP, tl), F32)
    for g in range(N_GROUPS):
        lo = SUBLANES + g * EXPERTS_PER_GROUP
        el = jnp.where(g_sel == g, logit[lo:lo + EXPERTS_PER_GROUP], el)
    v1 = jnp.max(el, axis=0, keepdims=True)
    i1 = jnp.min(jnp.where(el == v1, row, SUBLANES), axis=0, keepdims=True)
    el2 = jnp.where(row == i1, neg, el)
    v2 = jnp.max(el2, axis=0, keepdims=True)
    i2 = jnp.min(jnp.where(el2 == v2, row, SUBLANES), axis=0, keepdims=True)
    d = jnp.exp(v2 - v1)
    w1 = (1.0 / (1.0 + d)) * g_w
    w2 = (d / (1.0 + d)) * g_w
    e1 = (g_sel * EXPERTS_PER_GROUP + i1).astype(F32)
    e2 = (g_sel * EXPERTS_PER_GROUP + i2).astype(F32)
    rinfo_ref[...] = jnp.where(row == 0, e1, jnp.where(row == 1, e2, jnp.where(
        row == 2, w1, jnp.where(row == 3, w2, 0.0))))


def _mixer_tail(x, mix, mod_ref, rows_per_seq, n2g_ref, rw_ref, rb_ref, xmid_ref, h2_ref, rinfo_ref):
    gate1 = _seq_rows(mod_ref, 2 * D_MODEL, rows_per_seq)
    shift2 = _seq_rows(mod_ref, 3 * D_MODEL, rows_per_seq)
    scale2 = _seq_rows(mod_ref, 4 * D_MODEL, rows_per_seq)
    xmid = x + gate1 * mix
    xmid_ref[...] = xmid
    h2 = _rms(xmid, n2g_ref[...]) * (1.0 + scale2) + shift2
    _store_packed_rows(h2_ref, h2, x.shape[0])
    _route(h2, rw_ref, rb_ref, rinfo_ref)


def _premix(x, mod_ref, rows_per_seq, n1g_ref):
    shift1 = _seq_rows(mod_ref, 0, rows_per_seq)
    scale1 = _seq_rows(mod_ref, D_MODEL, rows_per_seq)
    return (_rms(x, n1g_ref[...]) * (1.0 + scale1) + shift1).astype(BF16)


def _mod_spec(per_row, seq_len):
    if per_row:
        return pl.BlockSpec((TL // seq_len, N_MOD * D_MODEL), lambda i: (i, 0))
    tiles_per_seq = seq_len // TL
    return pl.BlockSpec((None, 1, N_MOD * D_MODEL), lambda i: (i // tiles_per_seq, 0, 0))


def _gate2_spec(per_row, seq_len):
    if per_row:
        return pl.BlockSpec((TL // seq_len, D_MODEL), lambda i: (i, N_MOD - 1))
    tiles_per_seq = seq_len // TL
    return pl.BlockSpec((None, 1, D_MODEL), lambda i: (i // tiles_per_seq, 0, N_MOD - 1))


def _const_spec(shape):
    nd = len(shape)
    return pl.BlockSpec(shape, lambda i: (0,) * nd)


def _slot_specs(tile0, n_tiles_all):
    return [pl.BlockSpec((TL * ROW_SLABS, LANES), lambda i: (tile0 + i, 0)),
            pl.BlockSpec((TL * ROW_SLABS, LANES), lambda i: (n_tiles_all + tile0 + i, 0))]


def _mixer_out(t_rows, extra):
    shapes = [jax.ShapeDtypeStruct((t_rows, D_MODEL), F32),
              jax.ShapeDtypeStruct((t_rows * PACK_SLABS, LANES), U32),
              jax.ShapeDtypeStruct((SUBLANES, t_rows), F32)]
    specs = [pl.BlockSpec((TL, D_MODEL), lambda i: (i, 0)),
             pl.BlockSpec((TL * PACK_SLABS, LANES), lambda i: (i, 0)),
             pl.BlockSpec((SUBLANES, TL), lambda i: (0, i))]
    return shapes + [s for s, _ in extra], specs + [b for _, b in extra]


def _gmlp_kernel(x_ref, mod_ref, n1g_ref, n2g_ref, win_ref, lng_ref, lnb_ref, ws_ref, bs_ref,
                 wout_ref, rw_ref, rb_ref, *rest, seg, rows_per_seq, emit_v):
    if emit_v:
        xmid_ref, h2_ref, rinfo_ref, vout_ref, v_scr, p_scr = rest
    else:
        xmid_ref, h2_ref, rinfo_ref, v_scr, p_scr = rest
    tl = x_ref.shape[0]
    x = x_ref[...]
    hb = _premix(x, mod_ref, rows_per_seq, n1g_ref)

    cw = 512
    for c in range(A_WIDTH // cw):
        lo = A_WIDTH + c * cw
        v_scr[:, c * cw:(c + 1) * cw] = _gelu_tanh(_dot(hb, win_ref[:, lo:lo + cw]))
    v = v_scr[...]
    mu = jnp.mean(v, axis=-1, keepdims=True)
    vc = v - mu
    var = jnp.mean(vc * vc, axis=-1, keepdims=True)
    vn = vc * lax.rsqrt(var + EPS) * lng_ref[...] + lnb_ref[...]
    if emit_v:
        vout_ref[...] = vn
    vb = vn.astype(BF16)

    mt = ws_ref.shape[1]
    r = lax.broadcasted_iota(I32, (mt, mt), 0)
    c_ = lax.broadcasted_iota(I32, (mt, mt), 1)
    shift = seg.bit_length() - 1
    keep = ((r >> shift) == (c_ >> shift)) & (c_ <= r)
    for g in range(A_GROUPS):
        lo = g * A_GROUP_DIM
        m = jnp.where(keep, ws_ref[g], 0.0).astype(BF16)
        col = jnp.broadcast_to(bs_ref[0:seg, g:g + 1], (seg, A_GROUP_DIM))
        bias = jnp.concatenate([col] * (mt // seg), axis=0)
        s = jnp.concatenate(
            [_dot(m, vb[b * mt:(b + 1) * mt, lo:lo + A_GROUP_DIM]) + bias for b in range(tl // mt)], axis=0)
        u = _gelu_tanh(_dot(hb, win_ref[:, lo:lo + A_GROUP_DIM]))
        p_scr[:, lo:lo + A_GROUP_DIM] = (u * s).astype(BF16)
    mix = _dot(p_scr[...], wout_ref[...])
    _mixer_tail(x, mix, mod_ref, rows_per_seq, n2g_ref, rw_ref, rb_ref, xmid_ref, h2_ref, rinfo_ref)


def _gmlp(x, mod, per_row, seq_len, n1g, n2g, win, lng, lnb, ws, bs, wout, rw, rb, seg, emit_v):
    t_rows = x.shape[0]
    extra = []
    if emit_v:
        extra.append((jax.ShapeDtypeStruct((t_rows, A_WIDTH), F32),
                      pl.BlockSpec((TL, A_WIDTH), lambda i: (i, 0))))
    out_shape, out_specs = _mixer_out(t_rows, extra)
    return pl.pallas_call(
        functools.partial(_gmlp_kernel, seg=seg, rows_per_seq=seq_len, emit_v=emit_v),
        out_shape=out_shape,
        grid=(t_rows // TL,),
        in_specs=[
            pl.BlockSpec((TL, D_MODEL), lambda i: (i, 0)),
            _mod_spec(per_row, seq_len),
            _const_spec((1, D_MODEL)), _const_spec((1, D_MODEL)),
            _const_spec((D_MODEL, 2 * A_WIDTH)),
            _const_spec((1, A_WIDTH)), _const_spec((1, A_WIDTH)),
            _const_spec(ws.shape), _const_spec(bs.shape),
            _const_spec((A_WIDTH, D_MODEL)),
            _const_spec((ROUTER_ROWS, D_MODEL)), _const_spec((ROUTER_ROWS, LANES)),
        ],
        out_specs=out_specs,
        scratch_shapes=[pltpu.VMEM((TL, A_WIDTH), F32), pltpu.VMEM((TL, A_WIDTH), BF16)],
        compiler_params=pltpu.CompilerParams(
            dimension_semantics=("arbitrary",), vmem_limit_bytes=VMEM_LIMIT),
        name="gmlp_mixer",
    )(x, mod, n1g, n2g, win, lng, lnb, ws, bs, wout, rw, rb)


def _conv_kernel(xmid0_ref, gate0_ref, rinfo0_ref, o0_ref, o1_ref, mod_ref, n1g_ref, n2g_ref, win_ref,
                 cw_ref, wout_ref, rw_ref, rb_ref, *rest, tiles_per_seq, rows_per_seq, per_row):
    if per_row:
        p1_ref, p2_ref, xmid_ref, h2_ref, rinfo_ref, z_ref, zbuf = rest
    else:
        xmid_ref, h2_ref, rinfo_ref, z_ref, zbuf = rest
    tl = xmid0_ref.shape[0]
    x = _moe_residual(xmid0_ref, _seq_rows(gate0_ref, 0, rows_per_seq), rinfo0_ref, o0_ref, o1_ref)
    hb = _premix(x, mod_ref, rows_per_seq, n1g_ref)
    bg = _dot(hb, win_ref[:, 0:B_WIDTH])
    cg = _dot(hb, win_ref[:, B_WIDTH:2 * B_WIDTH])
    hx = _dot(hb, win_ref[:, 2 * B_WIDTH:3 * B_WIDTH])
    z = cg * hx

    if per_row:
        zbuf[0:SUBLANES, :] = jnp.zeros((SUBLANES, B_WIDTH), F32)
    else:
        @pl.when(pl.program_id(0) % tiles_per_seq == 0)
        def _():
            zbuf[0:SUBLANES, :] = jnp.zeros((SUBLANES, B_WIDTH), F32)
    zbuf[SUBLANES:SUBLANES + tl, :] = z
    sh1 = zbuf[SUBLANES - 1:SUBLANES - 1 + tl, :]
    sh2 = zbuf[SUBLANES - 2:SUBLANES - 2 + tl, :]
    if per_row:
        pos = lax.broadcasted_iota(I32, (tl, B_WIDTH), 0) & (SUBLANES - 1)
        sh1 = jnp.where(pos == 0, p1_ref[...], sh1)
        sh2 = jnp.where(pos < 2, p2_ref[...], sh2)
        z_ref[...] = z
    else:
        zbuf[0:SUBLANES, :] = z[tl - SUBLANES:tl, :]
        z_ref[...] = z[tl - SUBLANES:tl, :]
    conv = cw_ref[0:1, :] * sh2 + cw_ref[1:2, :] * sh1 + cw_ref[2:3, :] * z
    mix = _dot((bg * conv).astype(BF16), wout_ref[...])
    _mixer_tail(x, mix, mod_ref, rows_per_seq, n2g_ref, rw_ref, rb_ref, xmid_ref, h2_ref, rinfo_ref)


def _conv(xmid0, mod0, rinfo0, slots, tile0, n_tiles_all, mod, per_row, seq_len, n1g, n2g, win, cw, wout,
          rw, rb, prev1=None, prev2=None):
    t_rows = xmid0.shape[0]
    nt = t_rows // TL
    if per_row:
        extra = [(jax.ShapeDtypeStruct((t_rows, B_WIDTH), F32),
                  pl.BlockSpec((TL, B_WIDTH), lambda i: (i, 0)))]
    else:
        extra = [(jax.ShapeDtypeStruct((nt * SUBLANES, B_WIDTH), F32),
                  pl.BlockSpec((SUBLANES, B_WIDTH), lambda i: (i, 0)))]
    out_shape, out_specs = _mixer_out(t_rows, extra)
    in_specs = [
        pl.BlockSpec((TL, D_MODEL), lambda i: (i, 0)),
        _gate2_spec(per_row, seq_len),
        pl.BlockSpec((SUBLANES, TL), lambda i: (0, i)),
        *_slot_specs(tile0, n_tiles_all),
        _mod_spec(per_row, seq_len),
        _const_spec((1, D_MODEL)), _const_spec((1, D_MODEL)),
        _const_spec((D_MODEL, 3 * B_WIDTH)),
        _const_spec((CONV_W, B_WIDTH)),
        _const_spec((B_WIDTH, D_MODEL)),
        _const_spec((ROUTER_ROWS, D_MODEL)), _const_spec((ROUTER_ROWS, LANES)),
    ]
    args = [xmid0, mod0, rinfo0, slots, slots, mod, n1g, n2g, win, cw, wout, rw, rb]
    if per_row:
        in_specs += [pl.BlockSpec((TL, B_WIDTH), lambda i: (i, 0))] * 2
        args += [prev1, prev2]
    return pl.pallas_call(
        functools.partial(_conv_kernel, tiles_per_seq=max(seq_len // TL, 1), rows_per_seq=seq_len,
                          per_row=per_row),
        out_shape=out_shape,
        grid=(nt,),
        in_specs=in_specs,
        out_specs=out_specs,
        scratch_shapes=[pltpu.VMEM((TL + SUBLANES, B_WIDTH), F32)],
        compiler_params=pltpu.CompilerParams(
            dimension_semantics=("arbitrary",), vmem_limit_bytes=VMEM_LIMIT),
        name="conv_mixer",
    )(*args)


def _final_kernel(xmid_ref, gate_ref, rinfo_ref, o0_ref, o1_ref, fg_ref, out_ref, *, rows_per_seq):
    x = _moe_residual(xmid_ref, _seq_rows(gate_ref, 0, rows_per_seq), rinfo_ref, o0_ref, o1_ref)
    out_ref[...] = _rms(x, fg_ref[...])


def _final(xmid, mod, rinfo, slots, tile0, n_tiles_all, per_row, seq_len, final_g):
    t_rows = xmid.shape[0]
    return pl.pallas_call(
        functools.partial(_final_kernel, rows_per_seq=seq_len),
        out_shape=jax.ShapeDtypeStruct((t_rows, D_MODEL), F32),
        grid=(t_rows // TL,),
        in_specs=[
            pl.BlockSpec((TL, D_MODEL), lambda i: (i, 0)),
            _gate2_spec(per_row, seq_len),
            pl.BlockSpec((SUBLANES, TL), lambda i: (0, i)),
            *_slot_specs(tile0, n_tiles_all),
            _const_spec((1, D_MODEL)),
        ],
        out_specs=pl.BlockSpec((TL, D_MODEL), lambda i: (i, 0)),
        compiler_params=pltpu.CompilerParams(
            dimension_semantics=("arbitrary",), vmem_limit_bytes=VMEM_LIMIT),
        name="final_combine",
    )(xmid, mod, rinfo, slots, slots, final_g)


def _plan_kernel(rinfo_ref, pos_ref, cnt_ref, total, run, offs):
    phase = pl.program_id(0)
    b = pl.program_id(1)
    nb = pl.num_programs(1)
    tb = rinfo_ref.shape[1]
    e1 = rinfo_ref[0:1, :]
    e2 = rinfo_ref[1:2, :]
    erow = lax.broadcasted_iota(I32, (N_EXPERTS, tb), 0).astype(F32)
    hit1 = e1 == erow
    hit2 = e2 == erow
    member = jnp.where(hit1 | hit2, 1.0, 0.0)
    blk_cnt = jnp.broadcast_to(jnp.sum(member, axis=1, keepdims=True), (N_EXPERTS, LANES))

    @pl.when(phase == 0)
    def _():
        @pl.when(b == 0)
        def _():
            total[...] = jnp.zeros_like(total)
        total[...] += blk_cnt

        @pl.when(b == nb - 1)
        def _():
            cnt_ref[...] = total[...]

    @pl.when(phase == 1)
    def _():
        @pl.when(b == 0)
        def _():
            run[...] = jnp.zeros_like(run)
            padded = jnp.floor((total[...] + (TM - 1)) * (1.0 / TM)) * TM
            sub = lax.broadcasted_iota(I32, (N_EXPERTS, LANES), 0)
            lane = lax.broadcasted_iota(I32, (N_EXPERTS, LANES), 1)
            start_by_lane = jnp.sum(jnp.where(sub < lane, padded, 0.0), axis=0, keepdims=True)
            start = jnp.sum(jnp.where(lane == sub, start_by_lane, 0.0), axis=1, keepdims=True)
            offs[...] = jnp.broadcast_to(start, (N_EXPERTS, LANES))

        s_ = lax.broadcasted_iota(I32, (tb, tb), 0)
        t_ = lax.broadcasted_iota(I32, (tb, tb), 1)
        upper = jnp.where(s_ < t_, 1.0, 0.0).astype(BF16)
        before = _dot(member.astype(BF16), upper)
        dest = before + jnp.concatenate([run[...] + offs[...]] * (tb // LANES), axis=1)
        pos1 = jnp.sum(jnp.where(hit1, dest, 0.0), axis=0, keepdims=True)
        pos2 = jnp.sum(jnp.where(hit2, dest, 0.0), axis=0, keepdims=True)
        row = lax.broadcasted_iota(I32, (SUBLANES, tb), 0)
        pos_ref[...] = jnp.where(row == 0, pos1, jnp.where(row == 1, pos2, 0.0)).astype(I32)
        run[...] += blk_cnt


def _plan(rinfo):
    t = rinfo.shape[1]
    nb = t // TB
    return pl.pallas_call(
        _plan_kernel,
        out_shape=[jax.ShapeDtypeStruct((SUBLANES, t), I32),
                   jax.ShapeDtypeStruct((N_EXPERTS, LANES), F32)],
        grid=(2, nb),
        in_specs=[pl.BlockSpec((SUBLANES, TB), lambda p, b: (0, b))],
        out_specs=[pl.BlockSpec((SUBLANES, TB), lambda p, b: (0, b * p)),
                   pl.BlockSpec((N_EXPERTS, LANES), lambda p, b: (0, 0))],
        scratch_shapes=[pltpu.VMEM((N_EXPERTS, LANES), F32)] * 3,
        compiler_params=pltpu.CompilerParams(
            dimension_semantics=("arbitrary", "arbitrary"), vmem_limit_bytes=VMEM_LIMIT),
        name="moe_plan",
    )(rinfo)


def _spill_slot(tile, i, n_slots):
    return n_slots + (tile & 1) * TM + i


def _codes_kernel(pos_ref, code_ref, *, t_all, n_code_tiles, ti):
    b = pl.program_id(0)

    @pl.when(b == 0)
    def _():
        def init(ct, carry):
            for i in range(TM):
                code_ref[ct * TM + i] = _spill_slot(ct - 1, i, 2 * t_all)
            return carry
        lax.fori_loop(0, n_code_tiles, init, 0)

    unroll = 8

    def body(it, carry):
        for u in range(unroll):
            i = it * unroll + u
            tok = b * ti + i
            src = tok << CODE_SHIFT
            code_ref[TM + pos_ref[0, i]] = src | tok
            code_ref[TM + pos_ref[0, ti + i]] = src | (t_all + tok)
        return carry
    lax.fori_loop(0, ti // unroll, body, 0)


def _codes(pos, t_all, n_code_tiles):
    nblk = 8
    ti = t_all // nblk
    pos_blocks = jnp.transpose(pos[0:2].reshape(2, nblk, ti), (1, 0, 2)).reshape(nblk, 1, 2 * ti)
    return pl.pallas_call(
        functools.partial(_codes_kernel, t_all=t_all, n_code_tiles=n_code_tiles, ti=ti),
        out_shape=jax.ShapeDtypeStruct((n_code_tiles * TM,), I32),
        grid=(nblk,),
        in_specs=[pl.BlockSpec((None, 1, 2 * ti), lambda b: (b, 0, 0), memory_space=pltpu.SMEM)],
        out_specs=pl.BlockSpec(memory_space=pltpu.SMEM),
        compiler_params=pltpu.CompilerParams(
            dimension_semantics=("arbitrary",), vmem_limit_bytes=VMEM_LIMIT),
        name="moe_codes",
    )(pos_blocks)


def _expert_kernel(te_ref, nv_ref, code_prev_ref, code_ref, hp_ref, hs_ref, wg_ref, wu_ref, wd_ref,
                   slots_ref, hres, gbuf, ybuf, sem_in, sem_out, *, n_slots):
    j = pl.program_id(0)
    nvalid = nv_ref[j]
    par = j & 1
    tp4 = hp_ref.shape[0]
    ts4 = hs_ref.shape[0]
    tile_rows = TM * ROW_SLABS

    def tile_done(parity):
        return pltpu.make_async_copy(ybuf.at[pl.ds(0, tile_rows)], slots_ref.at[pl.ds(0, tile_rows)],
                                     sem_out.at[parity])

    def scatter_prev_tile():
        base = (1 - par) * tile_rows
        for i in range(TM):
            dst = code_prev_ref[0, i] & ((1 << CODE_SHIFT) - 1)
            pltpu.make_async_copy(
                ybuf.at[pl.ds(pl.multiple_of(base + i * ROW_SLABS, ROW_SLABS), ROW_SLABS)],
                slots_ref.at[pl.ds(pl.multiple_of(dst * ROW_SLABS, ROW_SLABS), ROW_SLABS)],
                sem_out.at[1 - par]).start()

    @pl.when(j == 0)
    def _():
        cp = pltpu.make_async_copy(hp_ref, hres.at[pl.ds(0, tp4)], sem_in)
        cs = pltpu.make_async_copy(hs_ref, hres.at[pl.ds(tp4, ts4)], sem_in)
        cp.start()
        cs.start()
        ybuf[...] = jnp.zeros(ybuf.shape, F32)
        spill = pltpu.make_async_copy(
            ybuf, slots_ref.at[pl.ds(n_slots * ROW_SLABS, 2 * tile_rows)], sem_in)
        spill.start()
        cp.wait()
        cs.wait()
        spill.wait()

    @pl.when(nvalid > 0)
    def _():
        for i in range(TM):
            tok = code_ref[0, i] >> CODE_SHIFT
            gbuf[pl.ds(i * PACK_SLABS, PACK_SLABS), :] = hres[
                pl.ds(pl.multiple_of(tok * PACK_SLABS, PACK_SLABS), PACK_SLABS), :]
        scatter_prev_tile()
        xb = _load_packed_rows(gbuf, TM).astype(BF16)
        hg = _dot(xb, wg_ref[...].astype(BF16))
        hu = _dot(xb, wu_ref[...].astype(BF16))
        hid = (hg * _sigmoid(hg) * hu).astype(BF16)
        y = _dot(hid, wd_ref[...].astype(BF16))

        @pl.when(j > 0)
        def _():
            tile_done(par).wait()
        _store_row_chunks(ybuf, y, TM, base=pl.multiple_of(par * tile_rows, tile_rows))

    @pl.when((nvalid == 0) & (nv_ref[jnp.maximum(j - 1, 0)] > 0))
    def _():
        tile_done(par).wait()
        scatter_prev_tile()
        tile_done(1 - par).wait()


def _experts(layer, tile_expert, tile_valid, codes, h2_p, h2_s, w_gate, w_up, w_down, n_slots):
    n_steps = tile_expert.shape[0]
    n_code_tiles = codes.shape[0]
    w_spec = pl.BlockSpec((None, None, D_MODEL, D_EXPERT), lambda j, te, nv: (layer, te[j], 0, 0))
    wd_spec = pl.BlockSpec((None, None, D_EXPERT, D_MODEL), lambda j, te, nv: (layer, te[j], 0, 0))
    t4 = h2_p.shape[0] + h2_s.shape[0]
    return pl.pallas_call(
        functools.partial(_expert_kernel, n_slots=n_slots),
        out_shape=jax.ShapeDtypeStruct(((n_slots + 2 * TM) * ROW_SLABS, LANES), F32),
        grid_spec=pltpu.PrefetchScalarGridSpec(
            num_scalar_prefetch=2,
            grid=(n_steps,),
            in_specs=[
                pl.BlockSpec((None, 1, TM), lambda j, te, nv: (j, 0, 0), memory_space=pltpu.SMEM),
                pl.BlockSpec((None, 1, TM), lambda j, te, nv: (jnp.minimum(j + 1, n_code_tiles - 1), 0, 0),
                             memory_space=pltpu.SMEM),
                pl.BlockSpec(memory_space=pl.ANY),
                pl.BlockSpec(memory_space=pl.ANY),
                w_spec, w_spec, wd_spec,
            ],
            out_specs=pl.BlockSpec(memory_space=pl.ANY),
            scratch_shapes=[
                pltpu.VMEM((t4, LANES), U32),
                pltpu.VMEM((TM * PACK_SLABS, LANES), U32),
                pltpu.VMEM((2 * TM * ROW_SLABS, LANES), F32),
                pltpu.SemaphoreType.DMA(()),
                pltpu.SemaphoreType.DMA((2,)),
            ],
        ),
        compiler_params=pltpu.CompilerParams(
            dimension_semantics=("arbitrary",), vmem_limit_bytes=EXPERT_VMEM_LIMIT,
            has_side_effects=True),
        name="moe_experts",
    )(tile_expert, tile_valid, codes, codes, h2_p, h2_s, w_gate, w_up, w_down)


def _tile_tables(counts, n_steps):
    cnt = counts[:, 0].astype(I32)
    tiles = (cnt + (TM - 1)) // TM
    ends = jnp.cumsum(tiles)
    starts = ends - tiles
    n_used = ends[-1]
    j = jnp.arange(n_steps, dtype=I32)
    jc = jnp.minimum(j, jnp.maximum(n_used - 1, 0))
    te = jnp.minimum(jnp.sum((ends[None, :] <= jc[:, None]).astype(I32), axis=1), N_EXPERTS - 1)
    local = jc - starts[te]
    nvalid = jnp.where(j < n_used, jnp.clip(cnt[te] - local * TM, 0, TM), 0)
    return te.astype(I32), nvalid.astype(I32)


def _router_params(w_group, b_group, w_router, b_router):
    rw = jnp.zeros((ROUTER_ROWS, D_MODEL), F32)
    rw = rw.at[0:N_GROUPS].set(w_group.T).at[SUBLANES:].set(w_router.T)
    rb = jnp.zeros((ROUTER_ROWS,), F32)
    rb = rb.at[0:N_GROUPS].set(b_group).at[SUBLANES:].set(b_router)
    return rw, jnp.broadcast_to(rb[:, None], (ROUTER_ROWS, LANES))


def _moe(layer, ri_p, ri_s, h2_p, h2_s, w_gate, w_up, w_down, t_all):
    n_tiles = (2 * t_all) // TM + N_EXPERTS
    rinfo = jnp.concatenate([ri_p, ri_s], axis=1)
    pos, counts = _plan(rinfo)
    te, nvalid = _tile_tables(counts, n_tiles + 1)
    codes = _codes(pos, t_all, n_tiles + 1).reshape(n_tiles + 1, 1, TM)
    return _experts(layer, te, nvalid, codes, h2_p, h2_s, w_gate, w_up, w_down, 2 * t_all)


def kernel(x_prompt, x_sample, c_prompt, c_sample, state_conv, ada_w, ada_b, norm1_g, norm2_g, final_g,
           a_w_in, a_ln_g, a_ln_b, a_w_s, a_b_s, a_w_out, b_w_in, b_conv_w, b_w_out,
           moe_w_group, moe_b_group, moe_w_router, moe_b_router, moe_w_gate, moe_w_up, moe_w_down):
    nb_p, seq_p, _ = x_prompt.shape
    nb_s, seq_s, _ = x_sample.shape
    tp = nb_p * seq_p
    ts = nb_s * seq_s
    t_all = tp + ts
    assert seq_p % TL == 0 and TL % seq_s == 0 and ts % TL == 0 and t_all % TB == 0 and t_all % 64 == 0
    assert seq_p % CHUNK == 0 and TL % CHUNK == 0 and seq_s == SUBLANES
    assert 2 * t_all + 2 * TM < (1 << CODE_SHIFT) and t_all < (1 << (31 - CODE_SHIFT))
    n_tiles_all = t_all // TL
    tile0_s = tp // TL

    mod = _ada(jnp.concatenate([c_prompt, c_sample], axis=0), ada_w, ada_b)
    mod_p = [mod[i, :nb_p].reshape(nb_p, 1, N_MOD * D_MODEL) for i in range(2)]
    mod_s = [mod[i, nb_p:] for i in range(2)]
    fg = final_g.reshape(1, D_MODEL)
    n1g = [norm1_g[i].reshape(1, D_MODEL) for i in range(2)]
    n2g = [norm2_g[i].reshape(1, D_MODEL) for i in range(2)]
    router = [_router_params(moe_w_group[i], moe_b_group[i], moe_w_router[i], moe_b_router[i])
              for i in range(2)]

    win = a_w_in[0].astype(BF16)
    wout = a_w_out[0].astype(BF16)
    lng = a_ln_g[0].reshape(1, A_WIDTH)
    lnb = a_ln_b[0].reshape(1, A_WIDTH)
    ws_s = jnp.tile(a_w_s[0][:, :seq_s, :seq_s], (1, TL // seq_s, TL // seq_s))
    bs = a_b_s[0].T
    xp = x_prompt.reshape(tp, D_MODEL)
    xs_ = x_sample.reshape(ts, D_MODEL)
    xmid_p, h2_p, ri_p = _gmlp(xp, mod_p[0], False, seq_p, n1g[0], n2g[0], win, lng, lnb, a_w_s[0], bs, wout,
                               *router[0], CHUNK, False)
    xmid_s, h2_s, ri_s, v_s = _gmlp(xs_, mod_s[0], True, seq_s, n1g[0], n2g[0], win, lng, lnb, ws_s, bs, wout,
                                    *router[0], seq_s, True)
    slots = _moe(0, ri_p, ri_s, h2_p, h2_s, moe_w_gate, moe_w_up, moe_w_down, t_all)

    win = b_w_in[0].astype(BF16)
    wout = b_w_out[0].astype(BF16)
    prev = state_conv[0]
    prev1 = jnp.repeat(prev[:, 1], seq_s, axis=0)
    first = (jnp.arange(ts) % seq_s == 0)[:, None]
    prev2 = jnp.where(first, jnp.repeat(prev[:, 0], seq_s, axis=0), prev1)
    xmid_p1, h2_p1, ri_p1, zt_p = _conv(xmid_p, mod_p[0], ri_p, slots, 0, n_tiles_all, mod_p[1], False, seq_p,
                                        n1g[1], n2g[1], win, b_conv_w[0], wout, *router[1])
    xmid_s1, h2_s1, ri_s1, z_s = _conv(xmid_s, mod_s[0], ri_s, slots, tile0_s, n_tiles_all, mod_s[1], True,
                                       seq_s, n1g[1], n2g[1], win, b_conv_w[0], wout, *router[1], prev1, prev2)
    conv_p = zt_p.reshape(nb_p, seq_p // TL, SUBLANES, B_WIDTH)[:, -1, SUBLANES - (CONV_W - 1):]
    conv_s = z_s.reshape(nb_s, seq_s, B_WIDTH)[:, seq_s - (CONV_W - 1):]
    slots = _moe(1, ri_p1, ri_s1, h2_p1, h2_s1, moe_w_gate, moe_w_up, moe_w_down, t_all)

    y_p = _final(xmid_p1, mod_p[1], ri_p1, slots, 0, n_tiles_all, False, seq_p, fg)
    y_s = _final(xmid_s1, mod_s[1], ri_s1, slots, tile0_s, n_tiles_all, True, seq_s, fg)
    return (y_p.reshape(nb_p, seq_p, D_MODEL), y_s.reshape(nb_s, seq_s, D_MODEL),
            conv_p[None], conv_s[None], v_s.reshape(nb_s, seq_s, A_WIDTH)[None])
```
